```python
import jax
import jax.numpy as jnp
from jax import lax
import numpy as np

D_MODEL = 1024
BATCH = 8
SEQ = 4096
DEPTH = 4

N_MIXERS = 4
EPS = 1e-6
D_FF = 4 * D_MODEL
CONV_WIDTH = 31
POOL_WINDOWS = (2, 4, 8, 16)
N_POOL_GROUPS = len(POOL_WINDOWS)
POOL_GROUP_DIM = D_MODEL // N_POOL_GROUPS
SHORT_CONV_WIDTH = 3
RET_HEADS = 4
RET_QK_DIM = D_MODEL // RET_HEADS
RET_V_DIM = 2 * D_MODEL // RET_HEADS
RET_QK_TOTAL = RET_HEADS * RET_QK_DIM
RET_V_TOTAL = RET_HEADS * RET_V_DIM
RET_CHUNK = 128
ROPE_BASE = 10000.0

kernel_name = "hybrid_interleaved_conv_pool_shortconv_retention"


def rms_norm(x, g):
    xf = x.astype(jnp.float32)
    y = xf * lax.rsqrt(jnp.mean(xf * xf, axis=-1, keepdims=True) + EPS)
    return (y * g.astype(jnp.float32)).astype(x.dtype)


def layer_norm(x, g, b):
    xf = x.astype(jnp.float32)
    mu = jnp.mean(xf, axis=-1, keepdims=True)
    xc = xf - mu
    y = xc * lax.rsqrt(jnp.mean(xc * xc, axis=-1, keepdims=True) + EPS)
    return (y * g.astype(jnp.float32) + b.astype(jnp.float32)).astype(x.dtype)


def causal_depthwise_conv(x, w):
    width = w.shape[0]
    return lax.conv_general_dilated(
        x, w[:, None, :].astype(x.dtype), window_strides=(1,),
        padding=[(width - 1, 0)], dimension_numbers=('NWC', 'WIO', 'NWC'),
        feature_group_count=x.shape[-1])


def conformer_conv(h, w_in, b_in, w_dw, b_dw, ln_g, ln_b, w_out, b_out):
    u = h @ w_in + b_in
    a, gate = jnp.split(u, 2, axis=-1)
    u = a * jax.nn.sigmoid(gate)
    u = causal_depthwise_conv(u, w_dw) + b_dw
    u = jax.nn.silu(layer_norm(u, ln_g, ln_b))
    return u @ w_out + b_out


def multiscale_pool(h, w_group, scale):
    b, s, d = h.shape
    hg = h.reshape(b, s, N_POOL_GROUPS, POOL_GROUP_DIM)
    hf = hg.astype(jnp.float32)
    cs = jnp.cumsum(hf, axis=1)
    t = jnp.arange(1, s + 1, dtype=jnp.float32)
    means = []
    for g, win in enumerate(POOL_WINDOWS):
        csg = cs[:, :, g]
        lagged = jnp.pad(csg, ((0, 0), (win, 0), (0, 0)))[:, :s]
        count = jnp.minimum(t, float(win))[None, :, None]
        means.append((csg - lagged) / count)
    pooled = jnp.stack(means, axis=2)
    mixed = (pooled - hf).astype(h.dtype)
    y = jnp.einsum('bsgc,gce->bsge', mixed, w_group).reshape(b, s, d)
    return y * scale


def short_gated_conv(h, w_in, w_dw, w_out):
    b_gate, c_gate, v = jnp.split(h @ w_in, 3, axis=-1)
    u = causal_depthwise_conv(c_gate * v, w_dw)
    return (b_gate * u) @ w_out


def rotary(x, positions):
    half = x.shape[-1] // 2
    inv_freq = ROPE_BASE ** (-jnp.arange(half, dtype=jnp.float32) / half)
    ang = positions.astype(jnp.float32)[..., None] * inv_freq
    cos = jnp.cos(ang)[:, :, None, :]
    sin = jnp.sin(ang)[:, :, None, :]
    x1, x2 = x[..., :half], x[..., half:]
    return jnp.concatenate([x1 * cos - x2 * sin, x1 * sin + x2 * cos], axis=-1)


def retention(h, positions, w_in, w_out):
    b, s, _ = h.shape
    n_chunks = s // RET_CHUNK
    q, k, v, g = jnp.split(h @ w_in, [RET_QK_TOTAL, 2 * RET_QK_TOTAL, 2 * RET_QK_TOTAL + RET_V_TOTAL], axis=-1)
    q = rotary(q.astype(jnp.float32).reshape(b, s, RET_HEADS, RET_QK_DIM), positions)
    k = rotary(k.astype(jnp.float32).reshape(b, s, RET_HEADS, RET_QK_DIM), positions) * (RET_QK_DIM ** -0.5)
    v = v.astype(jnp.float32).reshape(b, s, RET_HEADS, RET_V_DIM)

    def to_chunks(t):
        return t.reshape(b, n_chunks, RET_CHUNK, RET_HEADS, -1).transpose(1, 0, 3, 2, 4)

    qc, kc, vc = to_chunks(q), to_chunks(k), to_chunks(v)
    log_gamma = jnp.log1p(-jnp.exp2(-5.0 - jnp.arange(RET_HEADS, dtype=jnp.float32)))
    idx = jnp.arange(RET_CHUNK, dtype=jnp.float32)
    rel = idx[:, None] - idx[None, :]
    decay_mask = jnp.where(rel >= 0, jnp.exp(log_gamma[:, None, None] * jnp.maximum(rel, 0.0)), 0.0)
    q_decay = jnp.exp(log_gamma[:, None] * (idx + 1.0))[None, :, :, None]
    k_decay = jnp.exp(log_gamma[:, None] * (RET_CHUNK - 1.0 - idx))[None, :, :, None]
    chunk_decay = jnp.exp(log_gamma * RET_CHUNK)[None, :, None, None]

    def step(state, inp):
        q_n, k_n, v_n = inp
        scores = jnp.einsum('bhcd,bhmd->bhcm', q_n, k_n) * decay_mask[None]
        intra = jnp.einsum('bhcm,bhme->bhce', scores, v_n)
        cross = jnp.einsum('bhcd,bhde->bhce', q_n * q_decay, state)
        state = state * chunk_decay + jnp.einsum('bhcd,bhce->bhde', k_n * k_decay, v_n)
        return state, intra + cross

    state0 = jnp.zeros((b, RET_HEADS, RET_QK_DIM, RET_V_DIM), jnp.float32)
    _, o = lax.scan(step, state0, (qc, kc, vc))
    o = o.transpose(1, 0, 3, 2, 4).reshape(b, s, RET_HEADS, RET_V_DIM)
    o = o * lax.rsqrt(jnp.mean(o * o, axis=-1, keepdims=True) + EPS)
    o = o.reshape(b, s, RET_V_TOTAL).astype(h.dtype)
    return (jax.nn.silu(g) * o) @ w_out


def squared_relu_mlp(h, w_up, w_down):
    return jnp.square(jax.nn.relu(h @ w_up)) @ w_down


def setup_inputs(seed: int = 0) -> dict:
    key = jax.random.key(seed)
    ks = jax.random.split(key, 24)
    d = D_MODEL
    nrm = lambda k, shape, fan_in: jax.random.normal(k, shape, jnp.float32) * (fan_in ** -0.5)
    small = lambda k, shape: 0.02 * jax.random.normal(k, shape, jnp.float32)
    x = jax.random.normal(ks[0], (BATCH, SEQ, d), jnp.float32)
    positions = jnp.broadcast_to(jnp.arange(SEQ, dtype=jnp.int32)[None, :], (BATCH, SEQ))
    norm_g = 1.0 + 0.05 * jax.random.normal(ks[1], (DEPTH, 4, d), jnp.float32)
    mlp_up = nrm(ks[2], (DEPTH, d, D_FF), d)
    mlp_down = nrm(ks[3], (DEPTH, D_FF, d), D_FF)
    conv_w_in = nrm(ks[4], (d, 2 * d), d)
    conv_b_in = small(ks[5], (2 * d,))
    conv_dw = nrm(ks[6], (CONV_WIDTH, d), CONV_WIDTH)
    conv_dw_b = small(ks[7], (d,))
    conv_ln_g = 1.0 + 0.05 * jax.random.normal(ks[8], (d,), jnp.float32)
    conv_ln_b = small(ks[9], (d,))
    conv_w_out = nrm(ks[10], (d, d), d)
    conv_b_out = small(ks[11], (d,))
    pool_w = nrm(ks[12], (N_POOL_GROUPS, POOL_GROUP_DIM, POOL_GROUP_DIM), POOL_GROUP_DIM)
    pool_scale = 1.0 + 0.1 * jax.random.normal(ks[13], (d,), jnp.float32)
    sc_w_in = nrm(ks[14], (d, 3 * d), d)
    sc_dw = nrm(ks[15], (SHORT_CONV_WIDTH, d), SHORT_CONV_WIDTH)
    sc_w_out = nrm(ks[16], (d, d), d)
    ret_w_in = nrm(ks[17], (d, 2 * RET_QK_TOTAL + 2 * RET_V_TOTAL), d)
    ret_w_out = nrm(ks[18], (RET_V_TOTAL, d), RET_V_TOTAL)
    return {"x": x, "positions": positions, "norm_g": norm_g, "mlp_up": mlp_up, "mlp_down": mlp_down,
            "conv_w_in": conv_w_in, "conv_b_in": conv_b_in, "conv_dw": conv_dw, "conv_dw_b": conv_dw_b,
            "conv_ln_g": conv_ln_g, "conv_ln_b": conv_ln_b, "conv_w_out": conv_w_out, "conv_b_out": conv_b_out,
            "pool_w": pool_w, "pool_scale": pool_scale,
            "sc_w_in": sc_w_in, "sc_dw": sc_dw, "sc_w_out": sc_w_out,
            "ret_w_in": ret_w_in, "ret_w_out": ret_w_out}


def reference(x, positions, norm_g, mlp_up, mlp_down,
              conv_w_in, conv_b_in, conv_dw, conv_dw_b, conv_ln_g, conv_ln_b, conv_w_out, conv_b_out,
              pool_w, pool_scale,
              sc_w_in, sc_dw, sc_w_out,
              ret_w_in, ret_w_out):
    h = x
    for i in range(DEPTH):
        mixer = i % N_MIXERS
        u = rms_norm(h, norm_g[i, 0])
        if mixer == 0:
            u = conformer_conv(u, conv_w_in, conv_b_in, conv_dw, conv_dw_b, conv_ln_g, conv_ln_b, conv_w_out, conv_b_out)
        elif mixer == 1:
            u = multiscale_pool(u, pool_w, pool_scale)
        elif mixer == 2:
            u = short_gated_conv(u, sc_w_in, sc_dw, sc_w_out)
        else:
            u = retention(u, positions, ret_w_in, ret_w_out)
        h = h + rms_norm(u, norm_g[i, 1])
        u = squared_relu_mlp(rms_norm(h, norm_g[i, 2]), mlp_up[i], mlp_down[i])
        h = h + rms_norm(u, norm_g[i, 3])
    return h
```

```python
import functools

import numpy as np
import jax
import jax.numpy as jnp
from jax import lax
from jax.experimental import pallas as pl
from jax.experimental.pallas import tpu as pltpu

F32 = jnp.float32
BF16 = jnp.bfloat16

D_MODEL = 1024
D_FF = 4 * D_MODEL
EPS = 1e-6
CONV_WIDTH = 31
POOL_WINDOWS = (2, 4, 8, 16)
POOL_GROUP_DIM = D_MODEL // len(POOL_WINDOWS)
SHORT_CONV_WIDTH = 3
RET_HEADS = 4
RET_QK_DIM = D_MODEL // RET_HEADS
RET_V_DIM = 2 * D_MODEL // RET_HEADS
RET_QK_TOTAL = RET_HEADS * RET_QK_DIM
RET_V_TOTAL = RET_HEADS * RET_V_DIM
ROPE_BASE = 10000.0

SUBLANES = 8
LANES = 128
SEQ_TILE = 512
FF_CHUNK = 1024
RET_CHUNK = 128
CONV_HALO = 32
POOL_HALO = 16
SC_HALO = 8
ROW_BLOCK = 32
VMEM_LIMIT_BYTES = 56 * 1024 * 1024


def _rms(x, g):
    ms = jnp.mean(x * x, axis=-1, keepdims=True)
    return x * lax.rsqrt(ms + EPS) * g


def _dot(a, b):
    return jnp.dot(a, b, preferred_element_type=F32)


def _sigmoid(x):
    return 1.0 / (1.0 + jnp.exp(-x))


def _resident(shape):
    zeros = (0,) * len(shape)
    return pl.BlockSpec(shape, lambda b, s: zeros, pipeline_mode=pl.Buffered(1))


def _seq_spec(ts, width):
    return pl.BlockSpec((1, ts, width), lambda b, s: (b, s, 0))


def _params():
    return pltpu.CompilerParams(dimension_semantics=("arbitrary", "arbitrary"),
                                vmem_limit_bytes=VMEM_LIMIT_BYTES)


def _mlp_body(h_ref, g_ref, wup_ref, wdn_ref, o_ref):
    h = h_ref[0]
    xn = _rms(h, g_ref[0:1, :]).astype(BF16)
    acc = jnp.zeros(h.shape, F32)
    for c in range(D_FF // FF_CHUNK):
        cols = slice(c * FF_CHUNK, (c + 1) * FF_CHUNK)
        a = _dot(xn, wup_ref[:, cols])
        a = jnp.square(jnp.maximum(a, 0.0)).astype(BF16)
        acc = acc + _dot(a, wdn_ref[cols, :])
    o_ref[0] = h + _rms(acc, g_ref[1:2, :])


def _mlp(h, g, w_up, w_down, ts):
    b, s, d = h.shape
    return pl.pallas_call(
        _mlp_body,
        grid=(b, s // ts),
        in_specs=[_seq_spec(ts, d), _resident(g.shape), _resident(w_up.shape), _resident(w_down.shape)],
        out_specs=_seq_spec(ts, d),
        out_shape=jax.ShapeDtypeStruct(h.shape, h.dtype),
        compiler_params=_params(),
        name="mlp",
    )(h, g, w_up, w_down)


def _causal_taps(src_ref, dst_ref, w_ref, halo, n_rows, bias_ref=None):
    width = w_ref.shape[0]
    base = halo - (width - 1)
    n_sub = ROW_BLOCK // SUBLANES
    for c in range(src_ref.shape[1] // LANES):
        cols = slice(c * LANES, (c + 1) * LANES)
        taps = [jnp.broadcast_to(w_ref[k:k + 1, cols], (SUBLANES, LANES)) for k in range(width)]
        bias = None if bias_ref is None else jnp.broadcast_to(bias_ref[0:1, cols], (SUBLANES, LANES))

        def row_step(r, carry, cols=cols, taps=taps, bias=bias):
            r0 = pl.multiple_of(r * ROW_BLOCK, ROW_BLOCK)
            window = src_ref[pl.ds(r0, ROW_BLOCK + halo), cols]
            for u in range(n_sub):
                acc = None
                for k in range(width):
                    lo = u * SUBLANES + base + k
                    x = window[lo:lo + SUBLANES, :]
                    acc = taps[k] * x if acc is None else acc + taps[k] * x
                if bias is not None:
                    acc = acc + bias
                dst_ref[pl.ds(r0 + u * SUBLANES, SUBLANES), cols] = acc
            return carry

        lax.fori_loop(0, n_rows // ROW_BLOCK, row_step, 0)


def _conformer_body(h_ref, g_ref, win_ref, bin_ref, dw_ref, dwb_ref, lng_ref, lnb_ref, wout_ref, bout_ref,
                    o_ref, xbuf, cbuf):
    ts = h_ref.shape[1]
    d = h_ref.shape[2]

    @pl.when(pl.program_id(1) == 0)
    def _():
        xbuf[0:CONV_HALO, :] = jnp.zeros((CONV_HALO, d), F32)

    h = h_ref[0]
    xn = _rms(h, g_ref[0:1, :]).astype(BF16)
    a = _dot(xn, win_ref[:, 0:d]) + bin_ref[:, 0:d]
    gate = _dot(xn, win_ref[:, d:2 * d]) + bin_ref[:, d:2 * d]
    xbuf[CONV_HALO:CONV_HALO + ts, :] = a * _sigmoid(gate)

    _causal_taps(xbuf, cbuf, dw_ref, CONV_HALO, ts, bias_ref=dwb_ref)
    xbuf[0:CONV_HALO, :] = xbuf[ts:ts + CONV_HALO, :]

    c = cbuf[...]
    mu = jnp.mean(c, axis=-1, keepdims=True)
    xc = c - mu
    y = xc * lax.rsqrt(jnp.mean(xc * xc, axis=-1, keepdims=True) + EPS) * lng_ref[...] + lnb_ref[...]
    y = (y * _sigmoid(y)).astype(BF16)
    u = _dot(y, wout_ref[...]) + bout_ref[...]
    o_ref[0] = h + _rms(u, g_ref[1:2, :])


def _conformer(h, g, w_in, b_in, w_dw, b_dw, ln_g, ln_b, w_out, b_out, ts):
    b, s, d = h.shape
    consts = (g, w_in, b_in, w_dw, b_dw, ln_g, ln_b, w_out, b_out)
    return pl.pallas_call(
        _conformer_body,
        grid=(b, s // ts),
        in_specs=[_seq_spec(ts, d)] + [_resident(c.shape) for c in consts],
        out_specs=_seq_spec(ts, d),
        out_shape=jax.ShapeDtypeStruct(h.shape, h.dtype),
        scratch_shapes=[pltpu.VMEM((CONV_HALO + ts, d), F32), pltpu.VMEM((ts, d), F32)],
        compiler_params=_params(),
        name="conformer",
    )(h, *consts)


def _pool_body(h_ref, g_ref, pw_ref, scale_ref, o_ref, ubuf, pbuf):
    ts = h_ref.shape[1]
    d = h_ref.shape[2]
    seq_tile = pl.program_id(1)

    @pl.when(seq_tile == 0)
    def _():
        ubuf[0:POOL_HALO, :] = jnp.zeros((POOL_HALO, d), F32)

    h = h_ref[0]
    u = _rms(h, g_ref[0:1, :])
    ubuf[POOL_HALO:POOL_HALO + ts, :] = u

    for gi, win in enumerate(POOL_WINDOWS):
        for c in range(POOL_GROUP_DIM // LANES):
            lo = gi * POOL_GROUP_DIM + c * LANES
            cols = slice(lo, lo + LANES)

            def row_step(r, carry, cols=cols, win=win):
                r0 = pl.multiple_of(r * ROW_BLOCK, ROW_BLOCK)
                window = ubuf[pl.ds(r0, ROW_BLOCK + POOL_HALO), cols]
                acc = window[POOL_HALO:, :]
                for j in range(1, win):
                    acc = acc + window[POOL_HALO - j:POOL_HALO - j + ROW_BLOCK, :]
                pbuf[pl.ds(r0, ROW_BLOCK), cols] = acc
                return carry

            lax.fori_loop(0, ts // ROW_BLOCK, row_step, 0)
    ubuf[0:POOL_HALO, :] = ubuf[ts:ts + POOL_HALO, :]

    t1 = (seq_tile * ts + 1 + lax.broadcasted_iota(jnp.int32, (ts, 1), 0)).astype(F32)
    ys = []
    for gi, win in enumerate(POOL_WINDOWS):
        cols = slice(gi * POOL_GROUP_DIM, (gi + 1) * POOL_GROUP_DIM)
        count = jnp.minimum(t1, float(win))
        mixed = (pbuf[:, cols] / count - u[:, cols]).astype(BF16)
        ys.append(_dot(mixed, pw_ref[gi]))
    y = jnp.concatenate(ys, axis=-1) * scale_ref[...]
    o_ref[0] = h + _rms(y, g_ref[1:2, :])


def _pool(h, g, pool_w, pool_scale, ts):
    b, s, d = h.shape
    consts = (g, pool_w, pool_scale)
    return pl.pallas_call(
        _pool_body,
        grid=(b, s // ts),
        in_specs=[_seq_spec(ts, d)] + [_resident(c.shape) for c in consts],
        out_specs=_seq_spec(ts, d),
        out_shape=jax.ShapeDtypeStruct(h.shape, h.dtype),
        scratch_shapes=[pltpu.VMEM((POOL_HALO + ts, d), F32), pltpu.VMEM((ts, d), F32)],
        compiler_params=_params(),
        name="pool",
    )(h, *consts)


def _shortconv_body(h_ref, g_ref, win_ref, dw_ref, wout_ref, o_ref, xbuf, cbuf):
    ts = h_ref.shape[1]
    d = h_ref.shape[2]

    @pl.when(pl.program_id(1) == 0)
    def _():
        xbuf[0:SC_HALO, :] = jnp.zeros((SC_HALO, d), F32)

    h = h_ref[0]
    xn = _rms(h, g_ref[0:1, :]).astype(BF16)
    c_gate = _dot(xn, win_ref[:, d:2 * d])
    v = _dot(xn, win_ref[:, 2 * d:3 * d])
    xbuf[SC_HALO:SC_HALO + ts, :] = c_gate * v
    _causal_taps(xbuf, cbuf, dw_ref, SC_HALO, ts)
    xbuf[0:SC_HALO, :] = xbuf[ts:ts + SC_HALO, :]

    b_gate = _dot(xn, win_ref[:, 0:d])
    y = _dot((b_gate * cbuf[...]).astype(BF16), wout_ref[...])
    o_ref[0] = h + _rms(y, g_ref[1:2, :])


def _shortconv(h, g, w_in, w_dw, w_out, ts):
    b, s, d = h.shape
    consts = (g, w_in, w_dw, w_out)
    return pl.pallas_call(
        _shortconv_body,
        grid=(b, s // ts),
        in_specs=[_seq_spec(ts, d)] + [_resident(c.shape) for c in consts],
        out_specs=_seq_spec(ts, d),
        out_shape=jax.ShapeDtypeStruct(h.shape, h.dtype),
        scratch_shapes=[pltpu.VMEM((SC_HALO + ts, d), F32), pltpu.VMEM((ts, d), F32)],
        compiler_params=_params(),
        name="shortconv",
    )(h, *consts)


def _retention_tables(ts):
    c = RET_CHUNK
    log_gamma = np.log1p(-np.exp2(-5.0 - np.arange(RET_HEADS, dtype=np.float64)))
    idx = np.arange(c, dtype=np.float64)
    rel = idx[:, None] - idx[None, :]
    mask = np.where(rel >= 0, np.exp(log_gamma[:, None, None] * np.maximum(rel, 0.0)), 0.0)
    q_decay = np.exp(log_gamma[:, None] * (idx + 1.0))[:, :, None]
    k_decay = np.exp(log_gamma[:, None] * (c - 1.0 - idx))
    k_decay = np.tile(k_decay, (1, ts // c))[:, :, None]
    chunk_decay = tuple(float(v) for v in np.exp(log_gamma * c))
    as_f32 = lambda t: jnp.asarray(t.astype(np.float32))
    return as_f32(mask), as_f32(q_decay), as_f32(k_decay), chunk_decay


def _retention_body(chunk_decay, h_ref, pos_ref, g_ref, freq_ref, mask_ref, qdec_ref, kdec_ref, win_ref,
                    wout_ref, o_ref, state, xn_s, q_s, k_s, kd_s, v_s, og_s):
    ts = h_ref.shape[1]
    half = RET_QK_DIM // 2

    @pl.when(pl.program_id(1) == 0)
    def _():
        state[...] = jnp.zeros(state.shape, F32)

    h = h_ref[0]
    xn = _rms(h, g_ref[0:1, :]).astype(BF16)
    xn_s[...] = xn
    ang = pos_ref[0].astype(F32) * freq_ref[...]
    cos = jnp.cos(ang)
    sin = jnp.sin(ang)

    def rotate(z):
        z1, z2 = z[:, 0:half], z[:, half:2 * half]
        return z1 * cos - z2 * sin, z1 * sin + z2 * cos

    k_scale = RET_QK_DIM ** -0.5
    for hh in range(RET_HEADS):
        lo = hh * RET_QK_DIM
        q1, q2 = rotate(_dot(xn, win_ref[:, lo:lo + RET_QK_DIM]))
        q_s[:, lo:lo + half] = q1.astype(BF16)
        q_s[:, lo + half:lo + 2 * half] = q2.astype(BF16)
        k1, k2 = rotate(_dot(xn, win_ref[:, RET_QK_TOTAL + lo:RET_QK_TOTAL + lo + RET_QK_DIM]))
        k1, k2 = k1 * k_scale, k2 * k_scale
        k_s[:, lo:lo + half] = k1.astype(BF16)
        k_s[:, lo + half:lo + 2 * half] = k2.astype(BF16)
        kdec = kdec_ref[hh]
        kd_s[:, lo:lo + half] = (k1 * kdec).astype(BF16)
        kd_s[:, lo + half:lo + 2 * half] = (k2 * kdec).astype(BF16)
    v_lo = 2 * RET_QK_TOTAL
    g_lo = v_lo + RET_V_TOTAL
    v_s[...] = _dot(xn, win_ref[:, v_lo:v_lo + RET_V_TOTAL]).astype(BF16)

    def chunk_step(ci, carry):
        r0 = pl.multiple_of(ci * RET_CHUNK, RET_CHUNK)
        rows = pl.ds(r0, RET_CHUNK)
        xn_c = xn_s[rows, :]
        for hh in range(RET_HEADS):
            qk = slice(hh * RET_QK_DIM, (hh + 1) * RET_QK_DIM)
            vv = slice(hh * RET_V_DIM, (hh + 1) * RET_V_DIM)
            q_c = q_s[rows, qk]
            v_c = v_s[rows, vv]
            scores = lax.dot_general(q_c, k_s[rows, qk], (((1,), (1,)), ((), ())), preferred_element_type=F32)
            scores = (scores * mask_ref[hh]).astype(BF16)
            st = state[hh]
            o = _dot(scores, v_c) + _dot(q_c, st.astype(BF16)) * qdec_ref[hh]
            state[hh] = st * chunk_decay[hh] + lax.dot_general(
                kd_s[rows, qk], v_c, (((0,), (0,)), ((), ())), preferred_element_type=F32)
            o = o * lax.rsqrt(jnp.mean(o * o, axis=-1, keepdims=True) + EPS)
            gate = _dot(xn_c, win_ref[:, g_lo + hh * RET_V_DIM:g_lo + (hh + 1) * RET_V_DIM])
            og_s[rows, vv] = (gate * _sigmoid(gate) * o).astype(BF16)
        return carry

    lax.fori_loop(0, ts // RET_CHUNK, chunk_step, 0)

    y = _dot(og_s[...], wout_ref[...])
    o_ref[0] = h + _rms(y, g_ref[1:2, :])


def _retention(h, positions, g, w_in, w_out, ts):
    b, s, d = h.shape
    half = RET_QK_DIM // 2
    inv_freq = (ROPE_BASE ** (-jnp.arange(half, dtype=F32) / half)).reshape(1, half)
    mask, q_decay, k_decay, chunk_decay = _retention_tables(ts)
    pos = positions.reshape(b, s, 1)
    consts = (g, inv_freq, mask, q_decay, k_decay, w_in, w_out)
    return pl.pallas_call(
        functools.partial(_retention_body, chunk_decay),
        grid=(b, s // ts),
        in_specs=[_seq_spec(ts, d), _seq_spec(ts, 1)] + [_resident(c.shape) for c in consts],
        out_specs=_seq_spec(ts, d),
        out_shape=jax.ShapeDtypeStruct(h.shape, h.dtype),
        scratch_shapes=[
            pltpu.VMEM((RET_HEADS, RET_QK_DIM, RET_V_DIM), F32),
            pltpu.VMEM((ts, d), BF16),
            pltpu.VMEM((ts, RET_QK_TOTAL), BF16),
            pltpu.VMEM((ts, RET_QK_TOTAL), BF16),
            pltpu.VMEM((ts, RET_QK_TOTAL), BF16),
            pltpu.VMEM((ts, RET_V_TOTAL), BF16),
            pltpu.VMEM((ts, RET_V_TOTAL), BF16),
        ],
        compiler_params=_params(),
        name="retention",
    )(h, pos, *consts)


def kernel(x, positions, norm_g, mlp_up, mlp_down, conv_w_in, conv_b_in, conv_dw, conv_dw_b, conv_ln_g, conv_ln_b, conv_w_out, conv_b_out, pool_w, pool_scale, sc_w_in, sc_dw, sc_w_out, ret_w_in, ret_w_out):
    b, s, d = x.shape
    assert d == D_MODEL and norm_g.shape[0] % 4 == 0
    ts = min(SEQ_TILE, s)
    assert s % ts == 0 and ts % RET_CHUNK == 0 and ts % ROW_BLOCK == 0
    row = lambda v: v.reshape(1, -1)
    h = x
    for i in range(norm_g.shape[0]):
        g_mix, g_mlp = norm_g[i, 0:2], norm_g[i, 2:4]
        mixer = i % 4
        if mixer == 0:
            h = _conformer(h, g_mix, conv_w_in.astype(BF16), row(conv_b_in), conv_dw, row(conv_dw_b),
                           row(conv_ln_g), row(conv_ln_b), conv_w_out.astype(BF16), row(conv_b_out), ts)
        elif mixer == 1:
            h = _pool(h, g_mix, pool_w.astype(BF16), row(pool_scale), ts)
        elif mixer == 2:
            h = _shortconv(h, g_mix, sc_w_in.astype(BF16), sc_dw, sc_w_out.astype(BF16), ts)
        else:
            h = _retention(h, positions, g_mix, ret_w_in.astype(BF16), ret_w_out.astype(BF16), ts)
        h = _mlp(h, g_mlp, mlp_up[i].astype(BF16), mlp_down[i].astype(BF16), ts)
    return h
```

```python
import functools

import numpy as np
import jax
import jax.numpy as jnp
from jax import lax
from jax.experimental import pallas as pl
from jax.experimental.pallas import tpu as pltpu

F32 = jnp.float32
BF16 = jnp.bfloat16

D_MODEL = 1024
D_FF = 4 * D_MODEL
EPS = 1e-6
CONV_WIDTH = 31
POOL_WINDOWS = (2, 4, 8, 16)
POOL_GROUP_DIM = D_MODEL // len(POOL_WINDOWS)
SHORT_CONV_WIDTH = 3
RET_HEADS = 4
RET_QK_DIM = D_MODEL // RET_HEADS
RET_V_DIM = 2 * D_MODEL // RET_HEADS
RET_QK_TOTAL = RET_HEADS * RET_QK_DIM
RET_V_TOTAL = RET_HEADS * RET_V_DIM
ROPE_BASE = 10000.0

SUBLANES = 8
LANES = 128
SEQ_TILE = 512
MLP_TILE = 1024
FF_CHUNK = 1024
RET_CHUNK = 256
CONV_HALO = 32
POOL_HALO = 16
SC_HALO = 8
ROW_STRIDE = 4
ROW_BLOCK = SUBLANES * ROW_STRIDE
CONV_ROWS = 2 * ROW_BLOCK
VMEM_LIMIT_BYTES = 56 * 1024 * 1024


def _rms(x, g):
    ms = jnp.mean(x * x, axis=-1, keepdims=True)
    return x * lax.rsqrt(ms + EPS) * g


def _dot(a, b):
    return jnp.dot(a, b, preferred_element_type=F32)


def _sigmoid(x):
    return 1.0 / (1.0 + jnp.exp(-x))


def _resident(shape):
    zeros = (0,) * len(shape)
    return pl.BlockSpec(shape, lambda b, s: zeros, pipeline_mode=pl.Buffered(1))


def _seq_spec(ts, width):
    return pl.BlockSpec((1, ts, width), lambda b, s: (b, s, 0))


def _params():
    return pltpu.CompilerParams(dimension_semantics=("arbitrary", "arbitrary"),
                                vmem_limit_bytes=VMEM_LIMIT_BYTES)


def _mlp_body(h_ref, g_ref, wup_ref, wdn_ref, o_ref):
    h = h_ref[0]
    xn = _rms(h, g_ref[0:1, :]).astype(BF16)
    acc = jnp.zeros(h.shape, F32)
    for c in range(D_FF // FF_CHUNK):
        cols = slice(c * FF_CHUNK, (c + 1) * FF_CHUNK)
        a = _dot(xn, wup_ref[:, cols])
        a = jnp.square(jnp.maximum(a, 0.0)).astype(BF16)
        acc = acc + _dot(a, wdn_ref[cols, :])
    o_ref[0] = h + _rms(acc, g_ref[1:2, :])


def _mlp(h, g, w_up, w_down, ts):
    b, s, d = h.shape
    return pl.pallas_call(
        _mlp_body,
        grid=(b, s // ts),
        in_specs=[_seq_spec(ts, d), _resident(g.shape), _resident(w_up.shape), _resident(w_down.shape)],
        out_specs=_seq_spec(ts, d),
        out_shape=jax.ShapeDtypeStruct(h.shape, h.dtype),
        compiler_params=_params(),
        name="mlp",
    )(h, g, w_up, w_down)


def _slab_scratch(rows):
    return pltpu.VMEM((D_MODEL // LANES, rows, LANES), F32)


def _store_slabs(buf, row0, value):
    for c in range(buf.shape[0]):
        buf[c, row0:row0 + value.shape[0], :] = value[:, c * LANES:(c + 1) * LANES]


def _load_slabs(buf):
    return jnp.concatenate([buf[c] for c in range(buf.shape[0])], axis=-1)


def _carry_halo(buf, halo, n_rows):
    for c in range(buf.shape[0]):
        buf[c, 0:halo, :] = buf[c, n_rows:n_rows + halo, :]


def _strided_rows(start):
    return pl.ds(start, SUBLANES, stride=ROW_STRIDE)


def _causal_taps(src, dst, w_ref, halo, n_rows, bias_ref=None):
    width = w_ref.shape[1]
    base = halo - (width - 1)
    for c in range(src.shape[0]):

        def row_step(r, carry, c=c):
            r0 = pl.multiple_of(r * CONV_ROWS, CONV_ROWS)
            starts = [blk * ROW_BLOCK + u for blk in range(CONV_ROWS // ROW_BLOCK) for u in range(ROW_STRIDE)]
            n_parts = 2 if width > 8 else 1
            accs = [[None] * n_parts for _ in starts]
            for k in range(width):
                tap = w_ref[c, k]
                for j, st in enumerate(starts):
                    term = tap * src[c, _strided_rows(r0 + (st + base + k)), :]
                    prev = accs[j][k % n_parts]
                    accs[j][k % n_parts] = term if prev is None else prev + term
            bias = None if bias_ref is None else bias_ref[c, 0]
            for j, st in enumerate(starts):
                acc = accs[j][0] if n_parts == 1 else accs[j][0] + accs[j][1]
                if bias is not None:
                    acc = acc + bias
                dst[c, _strided_rows(r0 + st), :] = acc
            return carry

        lax.fori_loop(0, n_rows // CONV_ROWS, row_step, 0)


def _conformer_body(h_ref, g_ref, win_ref, bin_ref, dw_ref, dwb_ref, lng_ref, lnb_ref, wout_ref, bout_ref,
                    o_ref, xbuf, cbuf):
    ts = h_ref.shape[1]
    d = h_ref.shape[2]

    @pl.when(pl.program_id(1) == 0)
    def _():
        _store_slabs(xbuf, 0, jnp.zeros((CONV_HALO, d), F32))

    h = h_ref[0]
    xn = _rms(h, g_ref[0:1, :]).astype(BF16)
    a = _dot(xn, win_ref[:, 0:d]) + bin_ref[:, 0:d]
    gate = _dot(xn, win_ref[:, d:2 * d]) + bin_ref[:, d:2 * d]
    _store_slabs(xbuf, CONV_HALO, a * _sigmoid(gate))

    _causal_taps(xbuf, cbuf, dw_ref, CONV_HALO, ts, bias_ref=dwb_ref)
    _carry_halo(xbuf, CONV_HALO, ts)

    c = _load_slabs(cbuf)
    mu = jnp.mean(c, axis=-1, keepdims=True)
    xc = c - mu
    y = xc * lax.rsqrt(jnp.mean(xc * xc, axis=-1, keepdims=True) + EPS) * lng_ref[...] + lnb_ref[...]
    y = (y * _sigmoid(y)).astype(BF16)
    u = _dot(y, wout_ref[...]) + bout_ref[...]
    o_ref[0] = h + _rms(u, g_ref[1:2, :])


def _conformer(h, g, w_in, b_in, w_dw, b_dw, ln_g, ln_b, w_out, b_out, ts):
    b, s, d = h.shape
    consts = (g, w_in, b_in, w_dw, b_dw, ln_g, ln_b, w_out, b_out)
    return pl.pallas_call(
        _conformer_body,
        grid=(b, s // ts),
        in_specs=[_seq_spec(ts, d)] + [_resident(c.shape) for c in consts],
        out_specs=_seq_spec(ts, d),
        out_shape=jax.ShapeDtypeStruct(h.shape, h.dtype),
        scratch_shapes=[_slab_scratch(CONV_HALO + ts), _slab_scratch(ts)],
        compiler_params=_params(),
        name="conformer",
    )(h, *consts)


def _pool_body(h_ref, g_ref, pw_ref, scale_ref, o_ref, ubuf, pbuf):
    ts = h_ref.shape[1]
    d = h_ref.shape[2]
    seq_tile = pl.program_id(1)

    @pl.when(seq_tile == 0)
    def _():
        _store_slabs(ubuf, 0, jnp.zeros((POOL_HALO, d), F32))

    h = h_ref[0]
    u = _rms(h, g_ref[0:1, :])
    _store_slabs(ubuf, POOL_HALO, u)

    slabs_per_group = POOL_GROUP_DIM // LANES
    for c in range(d // LANES):
        win = POOL_WINDOWS[c // slabs_per_group]

        def row_step(r, carry, c=c, win=win):
            r0 = pl.multiple_of(r * ROW_BLOCK, ROW_BLOCK)
            for v in range(ROW_STRIDE):
                acc = ubuf[c, _strided_rows(r0 + (v + POOL_HALO)), :]
                for j in range(1, win):
                    acc = acc + ubuf[c, _strided_rows(r0 + (v + POOL_HALO - j)), :]
                pbuf[c, _strided_rows(r0 + v), :] = acc
            return carry

        lax.fori_loop(0, ts // ROW_BLOCK, row_step, 0)
    _carry_halo(ubuf, POOL_HALO, ts)

    t1 = (seq_tile * ts + 1 + lax.broadcasted_iota(jnp.int32, (ts, 1), 0)).astype(F32)
    ys = []
    for gi, win in enumerate(POOL_WINDOWS):
        cols = slice(gi * POOL_GROUP_DIM, (gi + 1) * POOL_GROUP_DIM)
        count = jnp.minimum(t1, float(win))
        sums = jnp.concatenate([pbuf[gi * slabs_per_group + j] for j in range(slabs_per_group)], axis=-1)
        mixed = (sums / count - u[:, cols]).astype(BF16)
        ys.append(_dot(mixed, pw_ref[gi]))
    y = jnp.concatenate(ys, axis=-1) * scale_ref[...]
    o_ref[0] = h + _rms(y, g_ref[1:2, :])


def _pool(h, g, pool_w, pool_scale, ts):
    b, s, d = h.shape
    consts = (g, pool_w, pool_scale)
    return pl.pallas_call(
        _pool_body,
        grid=(b, s // ts),
        in_specs=[_seq_spec(ts, d)] + [_resident(c.shape) for c in consts],
        out_specs=_seq_spec(ts, d),
        out_shape=jax.ShapeDtypeStruct(h.shape, h.dtype),
        scratch_shapes=[_slab_scratch(POOL_HALO + ts), _slab_scratch(ts)],
        compiler_params=_params(),
        name="pool",
    )(h, *consts)


def _shortconv_body(h_ref, g_ref, win_ref, dw_ref, wout_ref, o_ref, xbuf, cbuf):
    ts = h_ref.shape[1]
    d = h_ref.shape[2]

    @pl.when(pl.program_id(1) == 0)
    def _():
        _store_slabs(xbuf, 0, jnp.zeros((SC_HALO, d), F32))

    h = h_ref[0]
    xn = _rms(h, g_ref[0:1, :]).astype(BF16)
    c_gate = _dot(xn, win_ref[:, d:2 * d])
    v = _dot(xn, win_ref[:, 2 * d:3 * d])
    _store_slabs(xbuf, SC_HALO, c_gate * v)
    _causal_taps(xbuf, cbuf, dw_ref, SC_HALO, ts)
    _carry_halo(xbuf, SC_HALO, ts)

    b_gate = _dot(xn, win_ref[:, 0:d])
    y = _dot((b_gate * _load_slabs(cbuf)).astype(BF16), wout_ref[...])
    o_ref[0] = h + _rms(y, g_ref[1:2, :])


def _shortconv(h, g, w_in, w_dw, w_out, ts):
    b, s, d = h.shape
    consts = (g, w_in, w_dw, w_out)
    return pl.pallas_call(
        _shortconv_body,
        grid=(b, s // ts),
        in_specs=[_seq_spec(ts, d)] + [_resident(c.shape) for c in consts],
        out_specs=_seq_spec(ts, d),
        out_shape=jax.ShapeDtypeStruct(h.shape, h.dtype),
        scratch_shapes=[_slab_scratch(SC_HALO + ts), _slab_scratch(ts)],
        compiler_params=_params(),
        name="shortconv",
    )(h, *consts)


def _retention_tables(ts):
    c = RET_CHUNK
    log_gamma = np.log1p(-np.exp2(-5.0 - np.arange(RET_HEADS, dtype=np.float64)))
    idx = np.arange(c, dtype=np.float64)
    rel = idx[:, None] - idx[None, :]
    mask = np.where(rel >= 0, np.exp(log_gamma[:, None, None] * np.maximum(rel, 0.0)), 0.0)
    q_decay = np.exp(log_gamma[:, None] * (idx + 1.0))[:, :, None]
    k_decay = np.exp(log_gamma[:, None] * (c - 1.0 - idx))
    k_decay = np.tile(k_decay, (1, ts // c))[:, :, None]
    chunk_decay = tuple(float(v) for v in np.exp(log_gamma * c))
    as_f32 = lambda t: jnp.asarray(t.astype(np.float32))
    return as_f32(mask), as_f32(q_decay), as_f32(k_decay), chunk_decay


def _retention_body(chunk_decay, h_ref, pos_ref, g_ref, freq_ref, mask_ref, qdec_ref, kdec_ref, win_ref,
                    wout_ref, o_ref, state, q_s, k_s, kd_s, v_s, og_s):
    ts = h_ref.shape[1]
    half = RET_QK_DIM // 2

    @pl.when(pl.program_id(1) == 0)
    def _():
        state[...] = jnp.zeros(state.shape, F32)

    h = h_ref[0]
    xn = _rms(h, g_ref[0:1, :]).astype(BF16)
    ang = pos_ref[0].astype(F32) * freq_ref[...]
    cos = jnp.cos(ang)
    sin = jnp.sin(ang)

    def rotate(z):
        z1, z2 = z[:, 0:half], z[:, half:2 * half]
        return z1 * cos - z2 * sin, z1 * sin + z2 * cos

    k_scale = RET_QK_DIM ** -0.5
    for hh in range(RET_HEADS):
        lo = hh * RET_QK_DIM
        q1, q2 = rotate(_dot(xn, win_ref[:, lo:lo + RET_QK_DIM]))
        q_s[:, lo:lo + half] = q1.astype(BF16)
        q_s[:, lo + half:lo + 2 * half] = q2.astype(BF16)
        k1, k2 = rotate(_dot(xn, win_ref[:, RET_QK_TOTAL + lo:RET_QK_TOTAL + lo + RET_QK_DIM]))
        k1, k2 = k1 * k_scale, k2 * k_scale
        k_s[:, lo:lo + half] = k1.astype(BF16)
        k_s[:, lo + half:lo + 2 * half] = k2.astype(BF16)
        kdec = kdec_ref[hh]
        kd_s[:, lo:lo + half] = (k1 * kdec).astype(BF16)
        kd_s[:, lo + half:lo + 2 * half] = (k2 * kdec).astype(BF16)
    v_lo = 2 * RET_QK_TOTAL
    g_lo = v_lo + RET_V_TOTAL
    v_s[...] = _dot(xn, win_ref[:, v_lo:v_lo + RET_V_TOTAL]).astype(BF16)
    for hh in range(RET_HEADS):
        vv = slice(hh * RET_V_DIM, (hh + 1) * RET_V_DIM)
        gate = _dot(xn, win_ref[:, g_lo + hh * RET_V_DIM:g_lo + (hh + 1) * RET_V_DIM])
        og_s[:, vv] = (gate * _sigmoid(gate)).astype(BF16)

    for ci in range(ts // RET_CHUNK):
        rows = slice(ci * RET_CHUNK, (ci + 1) * RET_CHUNK)
        for hh in range(RET_HEADS):
            qk = slice(hh * RET_QK_DIM, (hh + 1) * RET_QK_DIM)
            vv = slice(hh * RET_V_DIM, (hh + 1) * RET_V_DIM)
            q_c = q_s[rows, qk]
            v_c = v_s[rows, vv]
            scores = lax.dot_general(q_c, k_s[rows, qk], (((1,), (1,)), ((), ())), preferred_element_type=F32)
            scores = (scores * mask_ref[hh]).astype(BF16)
            st = state[hh]
            o = _dot(scores, v_c) + _dot(q_c, st.astype(BF16)) * qdec_ref[hh]
            state[hh] = st * chunk_decay[hh] + lax.dot_general(
                kd_s[rows, qk], v_c, (((0,), (0,)), ((), ())), preferred_element_type=F32)
            o = o * lax.rsqrt(jnp.mean(o * o, axis=-1, keepdims=True) + EPS)
            og_s[rows, vv] = (og_s[rows, vv].astype(F32) * o).astype(BF16)

    y = _dot(og_s[...], wout_ref[...])
    o_ref[0] = h + _rms(y, g_ref[1:2, :])


def _retention(h, positions, g, w_in, w_out, ts):
    b, s, d = h.shape
    half = RET_QK_DIM // 2
    inv_freq = (ROPE_BASE ** (-jnp.arange(half, dtype=F32) / half)).reshape(1, half)
    mask, q_decay, k_decay, chunk_decay = _retention_tables(ts)
    pos = positions.reshape(b, s, 1)
    consts = (g, inv_freq, mask, q_decay, k_decay, w_in, w_out)
    return pl.pallas_call(
        functools.partial(_retention_body, chunk_decay),
        grid=(b, s // ts),
        in_specs=[_seq_spec(ts, d), _seq_spec(ts, 1)] + [_resident(c.shape) for c in consts],
        out_specs=_seq_spec(ts, d),
        out_shape=jax.ShapeDtypeStruct(h.shape, h.dtype),
        scratch_shapes=[
            pltpu.VMEM((RET_HEADS, RET_QK_DIM, RET_V_DIM), F32),
            pltpu.VMEM((ts, RET_QK_TOTAL), BF16),
            pltpu.VMEM((ts, RET_QK_TOTAL), BF16),
            pltpu.VMEM((ts, RET_QK_TOTAL), BF16),
            pltpu.VMEM((ts, RET_V_TOTAL), BF16),
            pltpu.VMEM((ts, RET_V_TOTAL), BF16),
        ],
        compiler_params=_params(),
        name="retention",
    )(h, pos, *consts)


def kernel(x, positions, norm_g, mlp_up, mlp_down, conv_w_in, conv_b_in, conv_dw, conv_dw_b, conv_ln_g, conv_ln_b, conv_w_out, conv_b_out, pool_w, pool_scale, sc_w_in, sc_dw, sc_w_out, ret_w_in, ret_w_out):
    b, s, d = x.shape
    assert d == D_MODEL and norm_g.shape[0] % 4 == 0
    ts = min(SEQ_TILE, s)
    assert s % ts == 0 and ts % RET_CHUNK == 0 and ts % CONV_ROWS == 0
    row = lambda v: v.reshape(1, -1)
    slabs = lambda v: jnp.broadcast_to(v.reshape(-1, d // LANES, 1, LANES).transpose(1, 0, 2, 3),
                                       (d // LANES, v.size // d, SUBLANES, LANES))
    h = x
    for i in range(norm_g.shape[0]):
        g_mix, g_mlp = norm_g[i, 0:2], norm_g[i, 2:4]
        mixer = i % 4
        if mixer == 0:
            h = _conformer(h, g_mix, conv_w_in.astype(BF16), row(conv_b_in), slabs(conv_dw), slabs(conv_dw_b),
                           row(conv_ln_g), row(conv_ln_b), conv_w_out.astype(BF16), row(conv_b_out), ts)
        elif mixer == 1:
            h = _pool(h, g_mix, pool_w.astype(BF16), row(pool_scale), ts)
        elif mixer == 2:
            h = _shortconv(h, g_mix, sc_w_in.astype(BF16), slabs(sc_dw), sc_w_out.astype(BF16), ts)
        else:
            h = _retention(h, positions, g_mix, ret_w_in.astype(BF16), ret_w_out.astype(BF16), ts)
        h = _mlp(h, g_mlp, mlp_up[i].astype(BF16), mlp_down[i].astype(BF16), min(MLP_TILE, s))
    return h
```

```python
import functools

import numpy as np
import jax
import jax.numpy as jnp
from jax import lax
from jax.experimental import pallas as pl
from jax.experimental.pallas import tpu as pltpu

F32 = jnp.float32
BF16 = jnp.bfloat16

D_MODEL = 1024
D_FF = 4 * D_MODEL
EPS = 1e-6
CONV_WIDTH = 31
POOL_WINDOWS = (2, 4, 8, 16)
POOL_GROUP_DIM = D_MODEL // len(POOL_WINDOWS)
SHORT_CONV_WIDTH = 3
RET_HEADS = 4
RET_QK_DIM = D_MODEL // RET_HEADS
RET_V_DIM = 2 * D_MODEL // RET_HEADS
RET_QK_TOTAL = RET_HEADS * RET_QK_DIM
RET_V_TOTAL = RET_HEADS * RET_V_DIM
ROPE_BASE = 10000.0

SUBLANES = 8
LANES = 128
SEQ_TILE = 512
MLP_TILE = 1024
FF_CHUNK = 1024
RET_CHUNK = 256
CONV_HALO = 32
POOL_HALO = 16
SC_HALO = 8
ROW_STRIDE = 4
ROW_BLOCK = SUBLANES * ROW_STRIDE
CONV_ROWS = 2 * ROW_BLOCK
FUSE_GROUPS = 4
FUSE_COLS = D_MODEL // FUSE_GROUPS
VMEM_LIMIT_BYTES = 56 * 1024 * 1024


def _rms(x, g):
    ms = jnp.mean(x * x, axis=-1, keepdims=True)
    return x * lax.rsqrt(ms + EPS) * g


def _dot(a, b):
    return jnp.dot(a, b, preferred_element_type=F32)


def _sigmoid(x):
    return 1.0 / (1.0 + jnp.exp(-x))


def _resident(shape, grid_rank):
    zeros = (0,) * len(shape)
    index_map = (lambda j: zeros) if grid_rank == 1 else (lambda b, s: zeros)
    return pl.BlockSpec(shape, index_map, pipeline_mode=pl.Buffered(1))


def _seq_spec(ts, width):
    return pl.BlockSpec((1, ts, width), lambda b, s: (b, s, 0))


def _params(grid_rank):
    return pltpu.CompilerParams(dimension_semantics=("arbitrary",) * grid_rank, vmem_limit_bytes=VMEM_LIMIT_BYTES)


def _mlp_body(h_ref, g_ref, wup_ref, wdn_ref, o_ref):
    h = h_ref[0]
    xn = _rms(h, g_ref[0:1, :]).astype(BF16)
    acc = jnp.zeros(h.shape, F32)
    for c in range(D_FF // FF_CHUNK):
        cols = slice(c * FF_CHUNK, (c + 1) * FF_CHUNK)
        a = _dot(xn, wup_ref[:, cols])
        a = jnp.square(jnp.maximum(a, 0.0)).astype(BF16)
        acc = acc + _dot(a, wdn_ref[cols, :])
    o_ref[0] = h + _rms(acc, g_ref[1:2, :])


def _mlp(h, g, w_up, w_down, ts):
    b, s, d = h.shape
    return pl.pallas_call(
        _mlp_body,
        grid=(b, s // ts),
        in_specs=[_seq_spec(ts, d)] + [_resident(c.shape, 2) for c in (g, w_up, w_down)],
        out_specs=_seq_spec(ts, d),
        out_shape=jax.ShapeDtypeStruct(h.shape, h.dtype),
        compiler_params=_params(2),
        name="mlp",
    )(h, g, w_up, w_down)


def _slab_scratch(rows):
    return pltpu.VMEM((D_MODEL // LANES, rows, LANES), F32)


def _store_slabs(buf, row0, value, first_slab=0):
    for c in range(value.shape[1] // LANES):
        buf[first_slab + c, row0:row0 + value.shape[0], :] = value[:, c * LANES:(c + 1) * LANES]


def _load_slabs(buf, first_slab=0, n_slabs=None):
    n_slabs = buf.shape[0] if n_slabs is None else n_slabs
    return jnp.concatenate([buf[first_slab + c] for c in range(n_slabs)], axis=-1)


def _carry_halo(buf, halo, n_rows):
    for c in range(buf.shape[0]):
        buf[c, 0:halo, :] = buf[c, n_rows:n_rows + halo, :]


def _strided_rows(start):
    return pl.ds(start, SUBLANES, stride=ROW_STRIDE)


def _conv_rows(src, dst, w_ref, bias_ref, c, r0, halo):
    width = w_ref.shape[1]
    base = halo - (width - 1)
    starts = [blk * ROW_BLOCK + u for blk in range(CONV_ROWS // ROW_BLOCK) for u in range(ROW_STRIDE)]
    n_parts = 2 if width > 8 else 1
    accs = [[None] * n_parts for _ in starts]
    for k in range(width):
        tap = w_ref[c, k]
        for j, st in enumerate(starts):
            term = tap * src[c, _strided_rows(r0 + (st + base + k)), :]
            prev = accs[j][k % n_parts]
            accs[j][k % n_parts] = term if prev is None else prev + term
    bias = None if bias_ref is None else bias_ref[c, 0]
    for j, st in enumerate(starts):
        acc = accs[j][0] if n_parts == 1 else accs[j][0] + accs[j][1]
        if bias is not None:
            acc = acc + bias
        dst[c, _strided_rows(r0 + st), :] = acc


def _conv_slab(src, dst, w_ref, bias_ref, c, halo, n_rows):
    for r in range(n_rows // CONV_ROWS):
        _conv_rows(src, dst, w_ref, bias_ref, c, r * CONV_ROWS, halo)


def _lag_begin(j, hmid_s, hprev_s, xn_s, gf_ref):
    @pl.when(j == 0)
    def _():
        hmid_s[...] = jnp.zeros(hmid_s.shape, F32)

    hp = hmid_s[...]
    hprev_s[...] = hp
    xn_s[...] = _rms(hp, gf_ref[0:1, :]).astype(BF16)


def _mlp_slice(gi, acc, xn_s, wup_ref, wdn_ref):
    ff_cols = D_FF // FUSE_GROUPS
    ff = slice(gi * ff_cols, (gi + 1) * ff_cols)
    a = jnp.square(jnp.maximum(_dot(xn_s[...], wup_ref[:, ff]), 0.0)).astype(BF16)
    part = _dot(a, wdn_ref[ff, :])
    return part if acc is None else acc + part


def _lag_end(o_ref, hmid_s, hprev_s, h, mixed, acc, gm_ref, gf_ref):
    hmid_s[...] = h + _rms(mixed, gm_ref[1:2, :])
    o_ref[0] = hprev_s[...] + _rms(acc, gf_ref[1:2, :])


def _fused_layer(body, name, h, consts, mixer_scratch, ts):
    b, s, d = h.shape
    tiles_per_seq = s // ts
    n_tiles = b * tiles_per_seq

    def in_map(j):
        t = jnp.minimum(j, n_tiles - 1)
        return (t // tiles_per_seq, t % tiles_per_seq, 0)

    def out_map(j):
        t = jnp.maximum(j - 1, 0)
        return (t // tiles_per_seq, t % tiles_per_seq, 0)

    lag_scratch = [pltpu.VMEM((ts, d), F32), pltpu.VMEM((ts, d), F32), pltpu.VMEM((ts, d), BF16)]
    return pl.pallas_call(
        functools.partial(body, n_tiles, tiles_per_seq),
        grid=(n_tiles + 1,),
        in_specs=[pl.BlockSpec((1, ts, d), in_map)] + [_resident(c.shape, 1) for c in consts],
        out_specs=pl.BlockSpec((1, ts, d), out_map),
        out_shape=jax.ShapeDtypeStruct(h.shape, h.dtype),
        scratch_shapes=lag_scratch + list(mixer_scratch),
        compiler_params=_params(1),
        name=name,
    )(h, *consts)


def _seq_tile(j, n_tiles, tiles_per_seq):
    return jnp.minimum(j, n_tiles - 1) % tiles_per_seq


def _conformer_mlp_body(n_tiles, tiles_per_seq, h_ref, gm_ref, win_ref, bin_ref, dw_ref, dwb_ref, lng_ref, lnb_ref,
                        wout_ref, bout_ref, gf_ref, wup_ref, wdn_ref, o_ref, hmid_s, hprev_s, xn_s, xbuf, cbuf):
    ts = h_ref.shape[1]
    d = h_ref.shape[2]
    j = pl.program_id(0)
    _lag_begin(j, hmid_s, hprev_s, xn_s, gf_ref)

    @pl.when(_seq_tile(j, n_tiles, tiles_per_seq) == 0)
    def _():
        _store_slabs(xbuf, 0, jnp.zeros((CONV_HALO, d), F32))

    h = h_ref[0]
    xm = _rms(h, gm_ref[0:1, :]).astype(BF16)
    slabs_per_group = FUSE_COLS // LANES
    acc = None
    for gi in range(FUSE_GROUPS):
        lo = gi * FUSE_COLS
        a = _dot(xm, win_ref[:, lo:lo + FUSE_COLS]) + bin_ref[:, lo:lo + FUSE_COLS]
        gate = _dot(xm, win_ref[:, d + lo:d + lo + FUSE_COLS]) + bin_ref[:, d + lo:d + lo + FUSE_COLS]
        _store_slabs(xbuf, CONV_HALO, a * _sigmoid(gate), first_slab=gi * slabs_per_group)
        for c in range(gi * slabs_per_group, (gi + 1) * slabs_per_group):
            _conv_slab(xbuf, cbuf, dw_ref, dwb_ref, c, CONV_HALO, ts)
        acc = _mlp_slice(gi, acc, xn_s, wup_ref, wdn_ref)
    _carry_halo(xbuf, CONV_HALO, ts)

    c = _load_slabs(cbuf)
    mu = jnp.mean(c, axis=-1, keepdims=True)
    xc = c - mu
    y = xc * lax.rsqrt(jnp.mean(xc * xc, axis=-1, keepdims=True) + EPS) * lng_ref[...] + lnb_ref[...]
    y = (y * _sigmoid(y)).astype(BF16)
    u = _dot(y, wout_ref[...]) + bout_ref[...]
    _lag_end(o_ref, hmid_s, hprev_s, h, u, acc, gm_ref, gf_ref)


def _pool_mlp_body(n_tiles, tiles_per_seq, h_ref, gm_ref, pw_ref, scale_ref, gf_ref, wup_ref, wdn_ref,
                   o_ref, hmid_s, hprev_s, xn_s, ubuf, pbuf):
    ts = h_ref.shape[1]
    d = h_ref.shape[2]
    j = pl.program_id(0)
    seq_tile = _seq_tile(j, n_tiles, tiles_per_seq)
    _lag_begin(j, hmid_s, hprev_s, xn_s, gf_ref)

    @pl.when(seq_tile == 0)
    def _():
        _store_slabs(ubuf, 0, jnp.zeros((POOL_HALO, d), F32))

    h = h_ref[0]
    u = _rms(h, gm_ref[0:1, :])
    _store_slabs(ubuf, POOL_HALO, u)
    t1 = (seq_tile * ts + 1 + lax.broadcasted_iota(jnp.int32, (ts, 1), 0)).astype(F32)
    slabs_per_group = POOL_GROUP_DIM // LANES
    acc = None
    ys = []
    for gi, win in enumerate(POOL_WINDOWS):
        for c in range(gi * slabs_per_group, (gi + 1) * slabs_per_group):
            for r0 in range(0, ts, ROW_BLOCK):
                for v in range(ROW_STRIDE):
                    total = ubuf[c, _strided_rows(r0 + v + POOL_HALO), :]
                    for back in range(1, win):
                        total = total + ubuf[c, _strided_rows(r0 + v + POOL_HALO - back), :]
                    pbuf[c, _strided_rows(r0 + v), :] = total
        cols = slice(gi * POOL_GROUP_DIM, (gi + 1) * POOL_GROUP_DIM)
        sums = _load_slabs(pbuf, gi * slabs_per_group, slabs_per_group)
        mixed = (sums / jnp.minimum(t1, float(win)) - u[:, cols]).astype(BF16)
        ys.append(_dot(mixed, pw_ref[gi]))
        acc = _mlp_slice(gi, acc, xn_s, wup_ref, wdn_ref)
    _carry_halo(ubuf, POOL_HALO, ts)
    y = jnp.concatenate(ys, axis=-1) * scale_ref[...]
    _lag_end(o_ref, hmid_s, hprev_s, h, y, acc, gm_ref, gf_ref)


def _shortconv_mlp_body(n_tiles, tiles_per_seq, h_ref, gm_ref, win_ref, dw_ref, wout_ref, gf_ref, wup_ref, wdn_ref,
                        o_ref, hmid_s, hprev_s, xn_s, xbuf, cbuf):
    ts = h_ref.shape[1]
    d = h_ref.shape[2]
    j = pl.program_id(0)
    _lag_begin(j, hmid_s, hprev_s, xn_s, gf_ref)

    @pl.when(_seq_tile(j, n_tiles, tiles_per_seq) == 0)
    def _():
        _store_slabs(xbuf, 0, jnp.zeros((SC_HALO, d), F32))

    h = h_ref[0]
    xm = _rms(h, gm_ref[0:1, :]).astype(BF16)
    slabs_per_group = FUSE_COLS // LANES
    acc = None
    gated = []
    for gi in range(FUSE_GROUPS):
        lo = gi * FUSE_COLS
        c_gate = _dot(xm, win_ref[:, d + lo:d + lo + FUSE_COLS])
        v = _dot(xm, win_ref[:, 2 * d + lo:2 * d + lo + FUSE_COLS])
        _store_slabs(xbuf, SC_HALO, c_gate * v, first_slab=gi * slabs_per_group)
        for c in range(gi * slabs_per_group, (gi + 1) * slabs_per_group):
            _conv_slab(xbuf, cbuf, dw_ref, None, c, SC_HALO, ts)
        b_gate = _dot(xm, win_ref[:, lo:lo + FUSE_COLS])
        gated.append((b_gate * _load_slabs(cbuf, gi * slabs_per_group, slabs_per_group)).astype(BF16))
        acc = _mlp_slice(gi, acc, xn_s, wup_ref, wdn_ref)
    _carry_halo(xbuf, SC_HALO, ts)
    y = _dot(jnp.concatenate(gated, axis=-1), wout_ref[...])
    _lag_end(o_ref, hmid_s, hprev_s, h, y, acc, gm_ref, gf_ref)


def _retention_tables(ts):
    c = RET_CHUNK
    log_gamma = np.log1p(-np.exp2(-5.0 - np.arange(RET_HEADS, dtype=np.float64)))
    idx = np.arange(c, dtype=np.float64)
    rel = idx[:, None] - idx[None, :]
    mask = np.where(rel >= 0, np.exp(log_gamma[:, None, None] * np.maximum(rel, 0.0)), 0.0)
    q_decay = np.exp(log_gamma[:, None] * (idx + 1.0))[:, :, None]
    k_decay = np.exp(log_gamma[:, None] * (c - 1.0 - idx))
    k_decay = np.tile(k_decay, (1, ts // c))[:, :, None]
    chunk_decay = tuple(float(v) for v in np.exp(log_gamma * c))
    as_f32 = lambda t: jnp.asarray(t.astype(np.float32))
    return as_f32(mask), as_f32(q_decay), as_f32(k_decay), chunk_decay


def _retention_body(chunk_decay, h_ref, pos_ref, g_ref, freq_ref, mask_ref, qdec_ref, kdec_ref, win_ref,
                    wout_ref, o_ref, state, q_s, k_s, kd_s, v_s, og_s):
    ts = h_ref.shape[1]
    half = RET_QK_DIM // 2

    @pl.when(pl.program_id(1) == 0)
    def _():
        state[...] = jnp.zeros(state.shape, F32)

    h = h_ref[0]
    xn = _rms(h, g_ref[0:1, :]).astype(BF16)
    ang = pos_ref[0].astype(F32) * freq_ref[...]
    cos = jnp.cos(ang)
    sin = jnp.sin(ang)

    def rotate(z):
        z1, z2 = z[:, 0:half], z[:, half:2 * half]
        return z1 * cos - z2 * sin, z1 * sin + z2 * cos

    k_scale = RET_QK_DIM ** -0.5
    for hh in range(RET_HEADS):
        lo = hh * RET_QK_DIM
        q1, q2 = rotate(_dot(xn, win_ref[:, lo:lo + RET_QK_DIM]))
        q_s[:, lo:lo + half] = q1.astype(BF16)
        q_s[:, lo + half:lo + 2 * half] = q2.astype(BF16)
        k1, k2 = rotate(_dot(xn, win_ref[:, RET_QK_TOTAL + lo:RET_QK_TOTAL + lo + RET_QK_DIM]))
        k1, k2 = k1 * k_scale, k2 * k_scale
        k_s[:, lo:lo + half] = k1.astype(BF16)
        k_s[:, lo + half:lo + 2 * half] = k2.astype(BF16)
        kdec = kdec_ref[hh]
        kd_s[:, lo:lo + half] = (k1 * kdec).astype(BF16)
        kd_s[:, lo + half:lo + 2 * half] = (k2 * kdec).astype(BF16)
    v_lo = 2 * RET_QK_TOTAL
    g_lo = v_lo + RET_V_TOTAL
    v_s[...] = _dot(xn, win_ref[:, v_lo:v_lo + RET_V_TOTAL]).astype(BF16)
    for hh in range(RET_HEADS):
        vv = slice(hh * RET_V_DIM, (hh + 1) * RET_V_DIM)
        gate = _dot(xn, win_ref[:, g_lo + hh * RET_V_DIM:g_lo + (hh + 1) * RET_V_DIM])
        og_s[:, vv] = (gate * _sigmoid(gate)).astype(BF16)

    for ci in range(ts // RET_CHUNK):
        rows = slice(ci * RET_CHUNK, (ci + 1) * RET_CHUNK)
        for hh in range(RET_HEADS):
            qk = slice(hh * RET_QK_DIM, (hh + 1) * RET_QK_DIM)
            vv = slice(hh * RET_V_DIM, (hh + 1) * RET_V_DIM)
            q_c = q_s[rows, qk]
            v_c = v_s[rows, vv]
            scores = lax.dot_general(q_c, k_s[rows, qk], (((1,), (1,)), ((), ())), preferred_element_type=F32)
            scores = (scores * mask_ref[hh]).astype(BF16)
            st = state[hh]
            o = _dot(scores, v_c) + _dot(q_c, st.astype(BF16)) * qdec_ref[hh]
            state[hh] = st * chunk_decay[hh] + lax.dot_general(
                kd_s[rows, qk], v_c, (((0,), (0,)), ((), ())), preferred_element_type=F32)
            o = o * lax.rsqrt(jnp.mean(o * o, axis=-1, keepdims=True) + EPS)
            og_s[rows, vv] = (og_s[rows, vv].astype(F32) * o).astype(BF16)

    y = _dot(og_s[...], wout_ref[...])
    o_ref[0] = h + _rms(y, g_ref[1:2, :])


def _retention(h, positions, g, w_in, w_out, ts):
    b, s, d = h.shape
    half = RET_QK_DIM // 2
    inv_freq = (ROPE_BASE ** (-jnp.arange(half, dtype=F32) / half)).reshape(1, half)
    mask, q_decay, k_decay, chunk_decay = _retention_tables(ts)
    pos = positions.reshape(b, s, 1)
    consts = (g, inv_freq, mask, q_decay, k_decay, w_in, w_out)
    return pl.pallas_call(
        functools.partial(_retention_body, chunk_decay),
        grid=(b, s // ts),
        in_specs=[_seq_spec(ts, d), _seq_spec(ts, 1)] + [_resident(c.shape, 2) for c in consts],
        out_specs=_seq_spec(ts, d),
        out_shape=jax.ShapeDtypeStruct(h.shape, h.dtype),
        scratch_shapes=[
            pltpu.VMEM((RET_HEADS, RET_QK_DIM, RET_V_DIM), F32),
            pltpu.VMEM((ts, RET_QK_TOTAL), BF16),
            pltpu.VMEM((ts, RET_QK_TOTAL), BF16),
            pltpu.VMEM((ts, RET_QK_TOTAL), BF16),
            pltpu.VMEM((ts, RET_V_TOTAL), BF16),
            pltpu.VMEM((ts, RET_V_TOTAL), BF16),
        ],
        compiler_params=_params(2),
        name="retention",
    )(h, pos, *consts)


def kernel(x, positions, norm_g, mlp_up, mlp_down, conv_w_in, conv_b_in, conv_dw, conv_dw_b, conv_ln_g, conv_ln_b, conv_w_out, conv_b_out, pool_w, pool_scale, sc_w_in, sc_dw, sc_w_out, ret_w_in, ret_w_out):
    b, s, d = x.shape
    assert d == D_MODEL and norm_g.shape[0] % 4 == 0
    ts = min(SEQ_TILE, s)
    assert s % ts == 0 and ts % RET_CHUNK == 0 and ts % CONV_ROWS == 0
    assert FUSE_GROUPS == len(POOL_WINDOWS) and FUSE_COLS == POOL_GROUP_DIM
    row = lambda v: v.reshape(1, -1)
    slabs = lambda v: jnp.broadcast_to(v.reshape(-1, d // LANES, 1, LANES).transpose(1, 0, 2, 3),
                                       (d // LANES, v.size // d, SUBLANES, LANES))
    h = x
    for i in range(norm_g.shape[0]):
        g_mix, g_mlp = norm_g[i, 0:2], norm_g[i, 2:4]
        mlp_consts = (g_mlp, mlp_up[i].astype(BF16), mlp_down[i].astype(BF16))
        mixer = i % 4
        if mixer == 0:
            consts = (g_mix, conv_w_in.astype(BF16), row(conv_b_in), slabs(conv_dw), slabs(conv_dw_b),
                      row(conv_ln_g), row(conv_ln_b), conv_w_out.astype(BF16), row(conv_b_out))
            h = _fused_layer(_conformer_mlp_body, "conformer_mlp", h, consts + mlp_consts,
                             [_slab_scratch(CONV_HALO + ts), _slab_scratch(ts)], ts)
        elif mixer == 1:
            consts = (g_mix, pool_w.astype(BF16), row(pool_scale))
            h = _fused_layer(_pool_mlp_body, "pool_mlp", h, consts + mlp_consts,
                             [_slab_scratch(POOL_HALO + ts), _slab_scratch(ts)], ts)
        elif mixer == 2:
            consts = (g_mix, sc_w_in.astype(BF16), slabs(sc_dw), sc_w_out.astype(BF16))
            h = _fused_layer(_shortconv_mlp_body, "shortconv_mlp", h, consts + mlp_consts,
                             [_slab_scratch(SC_HALO + ts), _slab_scratch(ts)], ts)
        else:
            h = _retention(h, positions, g_mix, ret_w_in.astype(BF16), ret_w_out.astype(BF16), ts)
            h = _mlp(h, *mlp_consts, min(MLP_TILE, s))
    return h
```
